```python
import jax
import jax.numpy as jnp
from jax import lax
import numpy as np

D_MODEL = 1024
BATCH = 4
SEQ = 4096
DEPTH = 2

GRID_W = 64
CTX_LEN = 256
HEAD_DIM = 64
RET_HEADS = 8
NA_HEADS = 8
RET_W = RET_HEADS * HEAD_DIM
NA_W = NA_HEADS * HEAD_DIM
MIX_W = RET_W + NA_W
AB_KV_W = 2 * RET_W + 2 * NA_W
AB_IN_W = AB_KV_W + 2 * RET_W + NA_W
RET_CHUNK = 128
NA_ROWS = 8
NA_COLS = 16
ROPE_BASE = 10000.0
SGU_CHUNK = 128
SGU_GROUPS = 8
SGU_W = 3 * D_MODEL
N_EXPERTS = 16
N_GROUPS = 4
EXPERTS_PER_GROUP = N_EXPERTS // N_GROUPS
GROUP_SCORE_K = 2
TOP_K = 2
EXPERT_W = 512
N_EVEN = (DEPTH + 1) // 2
N_ODD = DEPTH // 2
EPS = 1e-6

kernel_name = 'hybrid_retention_natten_sgu_moe_dit'


def rms_norm(x, g):
    xf = x.astype(jnp.float32)
    y = xf * lax.rsqrt(jnp.mean(xf * xf, axis=-1, keepdims=True) + EPS)
    return (y * g.astype(jnp.float32)).astype(x.dtype)


def modulate(h, shift, scale):
    return h * (1 + scale[:, None, :]) + shift[:, None, :]


def split_heads(t, n_heads):
    b, l, _ = t.shape
    return t.reshape(b, l, n_heads, HEAD_DIM).transpose(0, 2, 1, 3)


def merge_heads(t):
    b, h, l, d = t.shape
    return t.transpose(0, 2, 1, 3).reshape(b, l, h * d)


def rope_2d(t, rows, cols):
    half = HEAD_DIM // 2
    quarter = half // 2
    inv_freq = ROPE_BASE ** (-jnp.arange(quarter, dtype=jnp.float32) / quarter)

    def rotate(u, pos):
        ang = pos.astype(jnp.float32)[:, None] * inv_freq
        cos, sin = jnp.cos(ang).astype(u.dtype), jnp.sin(ang).astype(u.dtype)
        u1, u2 = u[..., :quarter], u[..., quarter:]
        return jnp.concatenate([u1 * cos - u2 * sin, u1 * sin + u2 * cos], axis=-1)

    return jnp.concatenate([rotate(t[..., :half], rows), rotate(t[..., half:], cols)], axis=-1)


def retention_chunkwise(q, k, v, log_gamma, s0):
    b, h, l, d = q.shape
    n = l // RET_CHUNK
    lg = log_gamma.astype(jnp.float32)
    pos = jnp.arange(RET_CHUNK, dtype=jnp.float32)
    diff = pos[:, None] - pos[None, :]
    intra = jnp.where(diff >= 0, jnp.exp(lg[:, None, None] * jnp.maximum(diff, 0.0)), 0.0)
    q_decay = jnp.exp(lg[:, None] * (pos + 1.0))[..., None]
    k_decay = jnp.exp(lg[:, None] * (RET_CHUNK - 1.0 - pos))[..., None]
    chunk_decay = jnp.exp(lg * RET_CHUNK)[:, None, None]

    def to_chunks(t):
        return t.astype(jnp.float32).reshape(b, h, n, RET_CHUNK, d).transpose(2, 0, 1, 3, 4)

    def step(s, qkv):
        qi, ki, vi = qkv
        scores = jnp.einsum('bhid,bhjd->bhij', qi, ki) * intra
        o = jnp.einsum('bhij,bhje->bhie', scores, vi) + jnp.einsum('bhid,bhde->bhie', qi, s) * q_decay
        s = s * chunk_decay + jnp.einsum('bhjd,bhje->bhde', ki * k_decay, vi)
        return s, o

    _, o = lax.scan(step, s0.astype(jnp.float32), (to_chunks(q), to_chunks(k), to_chunks(v)))
    return o.transpose(1, 2, 0, 3, 4).reshape(b, h, l, d).astype(q.dtype)


def retention_context_states(k, v, log_gamma):
    kf, vf = k.astype(jnp.float32), v.astype(jnp.float32)
    n = k.shape[2]
    m = jnp.arange(n, dtype=jnp.float32)
    w_fwd = jnp.exp(log_gamma[0][:, None] * (n - 1.0 - m))
    w_bwd = jnp.exp(log_gamma[1][:, None] * m)
    s_fwd = jnp.einsum('bhnd,hn,bhne->bhde', kf, w_fwd, vf)
    s_bwd = jnp.einsum('bhnd,hn,bhne->bhde', kf, w_bwd, vf)
    return s_fwd, s_bwd


def bidir_retention(q, k, v, log_gamma, s_fwd, s_bwd):
    fwd = retention_chunkwise(q, k, v, log_gamma[0], s_fwd)
    bwd = retention_chunkwise(q[:, :, ::-1], k[:, :, ::-1], v[:, :, ::-1], log_gamma[1], s_bwd)[:, :, ::-1]
    return fwd + bwd


def neighbourhood_attention(q, k, v, k_ctx, v_ctx, rpb):
    b, h, l, d = q.shape
    rows = l // GRID_W
    kr = min(NA_ROWS, rows)
    qg = q.reshape(b, h, rows, GRID_W, d)
    kg = k.reshape(b, h, rows, GRID_W, d)
    vg = v.reshape(b, h, rows, GRID_W, d)
    col_pos = jnp.arange(GRID_W)
    col_idx = jnp.clip(col_pos - NA_COLS // 2, 0, GRID_W - NA_COLS)[:, None] + jnp.arange(NA_COLS)
    col_bias_idx = col_idx - col_pos[:, None] + (NA_COLS - 1)
    n_loc = kr * NA_COLS

    def row_block(r):
        r0 = jnp.clip(r - kr // 2, 0, rows - kr)
        k_rows = lax.dynamic_slice_in_dim(kg, r0, kr, axis=2)
        v_rows = lax.dynamic_slice_in_dim(vg, r0, kr, axis=2)
        q_r = lax.dynamic_index_in_dim(qg, r, axis=2, keepdims=False)
        k_win = k_rows[:, :, :, col_idx]
        v_win = v_rows[:, :, :, col_idx]
        row_bias_idx = r0 + jnp.arange(kr) - r + (NA_ROWS - 1)
        bias = rpb[:, row_bias_idx][:, :, col_bias_idx].transpose(0, 2, 1, 3)
        s_loc = jnp.einsum('bhwd,bhrwcd->bhwrc', q_r, k_win) + bias
        s_ctx = jnp.einsum('bhwd,bhnd->bhwn', q_r, k_ctx)
        s = jnp.concatenate([s_loc.reshape(b, h, GRID_W, n_loc), s_ctx], axis=-1).astype(jnp.float32)
        p = jax.nn.softmax(s, axis=-1).astype(q.dtype)
        p_loc = p[..., :n_loc].reshape(b, h, GRID_W, kr, NA_COLS)
        p_ctx = p[..., n_loc:]
        return jnp.einsum('bhwrc,bhrwcd->bhwd', p_loc, v_win) + jnp.einsum('bhwn,bhnd->bhwd', p_ctx, v_ctx)

    out = lax.map(row_block, jnp.arange(rows))
    return out.transpose(1, 2, 0, 3, 4).reshape(b, h, l, d)


def retention_na_mixer(h, h_ctx, rows, cols, w_in, w_out, ret_theta, ret_g, q_g, k_g, rpb, ctx_out):
    scale = HEAD_DIM ** -0.5
    kv, rest = jnp.split(h @ w_in, [AB_KV_W], axis=-1)
    r_k, r_v, n_k, n_v = [split_heads(t, RET_HEADS) for t in jnp.split(kv, 4, axis=-1)]
    r_q, r_gate, n_q = jnp.split(rest, 3, axis=-1)
    proj_c = h_ctx @ (w_in if ctx_out else w_in[:, :AB_KV_W])
    rc_k, rc_v, nc_k, nc_v = [split_heads(t, RET_HEADS) for t in jnp.split(proj_c[..., :AB_KV_W], 4, axis=-1)]

    log_gamma = -jax.nn.softplus(ret_theta.astype(jnp.float32))
    rc_k = rc_k * scale
    s_fwd, s_bwd = retention_context_states(rc_k, rc_v, log_gamma)
    q = rope_2d(split_heads(r_q, RET_HEADS), rows, cols)
    k = rope_2d(r_k, rows, cols) * scale
    o = bidir_retention(q, k, r_v, log_gamma, s_fwd, s_bwd)
    ret = merge_heads(rms_norm(o, ret_g)) * jax.nn.silu(r_gate)

    kc_n = rms_norm(nc_k, k_g)
    qn = rms_norm(split_heads(n_q, NA_HEADS), q_g) * scale
    na = merge_heads(neighbourhood_attention(qn, rms_norm(n_k, k_g), n_v, kc_n, nc_v, rpb))
    y = jnp.concatenate([ret, na], axis=-1) @ w_out
    if not ctx_out:
        return y, None

    rc_q, rc_gate, nc_q = jnp.split(proj_c[..., AB_KV_W:], 3, axis=-1)
    zeros = jnp.zeros(s_fwd.shape, jnp.float32)
    o_c = bidir_retention(split_heads(rc_q, RET_HEADS), rc_k, rc_v, log_gamma, zeros, zeros)
    ret_c = merge_heads(rms_norm(o_c, ret_g)) * jax.nn.silu(rc_gate)
    qn_c = rms_norm(split_heads(nc_q, NA_HEADS), q_g) * scale
    p = jax.nn.softmax(jnp.einsum('bhqd,bhkd->bhqk', qn_c, kc_n).astype(jnp.float32), axis=-1).astype(h.dtype)
    na_c = merge_heads(jnp.einsum('bhqk,bhkd->bhqd', p, nc_v))
    y_ctx = jnp.concatenate([ret_c, na_c], axis=-1) @ w_out
    return y, y_ctx


def spatial_gating_unit(h, w_in, b_in, norm_g, w_s, b_s, w_out):
    b, l, _ = h.shape
    z = jax.nn.gelu(h @ w_in + b_in)
    u, v = jnp.split(z, 2, axis=-1)
    v = rms_norm(v, norm_g)
    vg = v.reshape(b, l // SGU_CHUNK, SGU_CHUNK, SGU_GROUPS, SGU_W // SGU_GROUPS)
    mixed = jnp.einsum('gij,bnjgc->bnigc', w_s, vg) + b_s.T[:, :, None]
    return (u * mixed.reshape(b, l, SGU_W)) @ w_out


def moe(h, router_w, router_bias, w_gate, w_up, w_down):
    affinity = jax.nn.sigmoid((h @ router_w).astype(jnp.float32))
    sel = affinity + router_bias.astype(jnp.float32)
    grouped = sel.reshape(*sel.shape[:-1], N_GROUPS, EXPERTS_PER_GROUP)
    group_score = lax.top_k(grouped, GROUP_SCORE_K)[0].sum(axis=-1)
    best = jnp.argmax(group_score, axis=-1)
    in_group = (jnp.arange(N_EXPERTS) // EXPERTS_PER_GROUP) == best[..., None]
    _, top_idx = lax.top_k(jnp.where(in_group, sel, -jnp.inf), TOP_K)
    w = jnp.take_along_axis(affinity, top_idx, axis=-1)
    w = w / jnp.sum(w, axis=-1, keepdims=True)
    combine = jnp.sum(jax.nn.one_hot(top_idx, N_EXPERTS, dtype=jnp.float32) * w[..., None], axis=-2).astype(h.dtype)
    out = jnp.zeros_like(h)
    for e in range(N_EXPERTS):
        y = (jax.nn.silu(h @ w_gate[e]) * (h @ w_up[e])) @ w_down[e]
        out = out + combine[..., e:e + 1] * y
    return out


def setup_inputs(seed: int = 0) -> dict:
    key = jax.random.key(seed)
    ks = jax.random.split(key, 26)
    f32 = jnp.float32
    D = D_MODEL

    def nrm(k, shape, scale):
        return jax.random.normal(k, shape, f32) * scale

    theta0 = np.log(np.expm1(-np.log(1.0 - 2.0 ** (-5.0 - np.arange(RET_HEADS))))).astype(np.float32)
    return {
        'x': nrm(ks[0], (BATCH, SEQ, D), 1.0),
        'c': nrm(ks[1], (BATCH, D), 1.0),
        'ctx': nrm(ks[2], (BATCH, CTX_LEN, D), 1.0),
        'c_ctx': nrm(ks[3], (D,), 1.0),
        'ada_w': nrm(ks[4], (DEPTH, D, 6 * D), 0.5 * D ** -0.5),
        'ada_b': nrm(ks[5], (DEPTH, 6 * D), 0.02),
        'norm_mix_g': 1.0 + nrm(ks[6], (DEPTH, D), 0.05),
        'norm_ffn_g': 1.0 + nrm(ks[7], (DEPTH, D), 0.05),
        'router_w': nrm(ks[8], (D, N_EXPERTS), D ** -0.5),
        'router_bias': nrm(ks[9], (N_EXPERTS,), 0.01),
        'moe_w_gate': nrm(ks[10], (DEPTH, N_EXPERTS, D, EXPERT_W), D ** -0.5),
        'moe_w_up': nrm(ks[11], (DEPTH, N_EXPERTS, D, EXPERT_W), D ** -0.5),
        'moe_w_down': nrm(ks[12], (DEPTH, N_EXPERTS, EXPERT_W, D), EXPERT_W ** -0.5),
        'ab_w_in': nrm(ks[13], (N_EVEN, D, AB_IN_W), D ** -0.5),
        'ab_w_out': nrm(ks[14], (N_EVEN, MIX_W, D), MIX_W ** -0.5),
        'ret_decay': jnp.asarray(theta0) + nrm(ks[15], (N_EVEN, 2, RET_HEADS), 0.05),
        'ret_norm_g': 1.0 + nrm(ks[16], (N_EVEN, HEAD_DIM), 0.05),
        'na_q_g': 1.0 + nrm(ks[17], (N_EVEN, HEAD_DIM), 0.05),
        'na_k_g': 1.0 + nrm(ks[18], (N_EVEN, HEAD_DIM), 0.05),
        'na_rpb': nrm(ks[19], (N_EVEN, NA_HEADS, 2 * NA_ROWS - 1, 2 * NA_COLS - 1), 0.02),
        'sgu_w_in': nrm(ks[20], (N_ODD, D, 2 * SGU_W), D ** -0.5),
        'sgu_b_in': nrm(ks[21], (N_ODD, 2 * SGU_W), 0.02),
        'sgu_norm_g': 1.0 + nrm(ks[22], (N_ODD, SGU_W), 0.05),
        'sgu_w_s': nrm(ks[23], (N_ODD, SGU_GROUPS, SGU_CHUNK, SGU_CHUNK), SGU_CHUNK ** -0.5),
        'sgu_b_s': 1.0 + nrm(ks[24], (N_ODD, SGU_GROUPS, SGU_CHUNK), 0.02),
        'sgu_w_out': nrm(ks[25], (N_ODD, SGU_W, D), SGU_W ** -0.5),
    }


def reference(x, c, ctx, c_ctx, ada_w, ada_b, norm_mix_g, norm_ffn_g, router_w, router_bias,
              moe_w_gate, moe_w_up, moe_w_down, ab_w_in, ab_w_out, ret_decay, ret_norm_g,
              na_q_g, na_k_g, na_rpb, sgu_w_in, sgu_b_in, sgu_norm_g, sgu_w_s, sgu_b_s, sgu_w_out):
    seq = x.shape[1]
    t = jnp.arange(seq)
    rows, cols = t // GRID_W, t % GRID_W
    silu_c = jax.nn.silu(c)
    silu_cc = jax.nn.silu(c_ctx)[None]
    for l in range(DEPTH):
        even = l % 2 == 0
        i = l // 2
        ctx_out = any(j % 2 == 0 for j in range(l + 1, DEPTH))
        sh_m, sc_m, g_m, sh_f, sc_f, g_f = jnp.split(silu_c @ ada_w[l] + ada_b[l], 6, axis=-1)
        h = modulate(rms_norm(x, norm_mix_g[l]), sh_m, sc_m)
        if even or ctx_out:
            n_mod = 6 if ctx_out else 2
            mod_c = jnp.split(silu_cc @ ada_w[l][:, :n_mod * D_MODEL] + ada_b[l][:n_mod * D_MODEL], n_mod, axis=-1)
            h_ctx = modulate(rms_norm(ctx, norm_mix_g[l]), mod_c[0], mod_c[1])
        if even:
            y, y_ctx = retention_na_mixer(h, h_ctx, rows, cols, ab_w_in[i], ab_w_out[i], ret_decay[i],
                                          ret_norm_g[i], na_q_g[i], na_k_g[i], na_rpb[i], ctx_out)
        else:
            sgu_args = (sgu_w_in[i], sgu_b_in[i], sgu_norm_g[i], sgu_w_s[i], sgu_b_s[i], sgu_w_out[i])
            y = spatial_gating_unit(h, *sgu_args)
            y_ctx = spatial_gating_unit(h_ctx, *sgu_args) if ctx_out else None
        x = x + g_m[:, None, :] * y
        h_f = modulate(rms_norm(x, norm_ffn_g[l]), sh_f, sc_f)
        x = x + g_f[:, None, :] * moe(h_f, router_w, router_bias, moe_w_gate[l], moe_w_up[l], moe_w_down[l])
        if ctx_out:
            ctx = ctx + mod_c[2][:, None, :] * y_ctx
            hc_f = modulate(rms_norm(ctx, norm_ffn_g[l]), mod_c[3], mod_c[4])
            ctx = ctx + mod_c[5][:, None, :] * moe(hc_f, router_w, router_bias, moe_w_gate[l], moe_w_up[l], moe_w_down[l])
    return x
```

```python
import functools

import jax
import jax.numpy as jnp
from jax import lax
from jax.experimental import pallas as pl
from jax.experimental.pallas import tpu as pltpu

D_MODEL = 1024
GRID_W = 64
HEAD_DIM = 64
RET_HEADS = 8
NA_HEADS = 8
RET_W = RET_HEADS * HEAD_DIM
NA_W = NA_HEADS * HEAD_DIM
AB_KV_W = 2 * RET_W + 2 * NA_W
AB_IN_W = AB_KV_W + 2 * RET_W + NA_W
RET_CHUNK = 128
NA_ROWS = 8
NA_COLS = 16
ROPE_BASE = 10000.0
SGU_CHUNK = 128
SGU_GROUPS = 8
SGU_W = 3 * D_MODEL
N_EXPERTS = 16
N_GROUPS = 4
EXPERTS_PER_GROUP = 4
EXPERT_W = 512
EPS = 1e-6

LANES = 128
HEAD_PAIR_W = 2 * HEAD_DIM
COL_TILE = 512
PACK_ROWS = D_MODEL // 2 // LANES
N_PAIRS = 6
N_BUCKETS = N_GROUPS * N_PAIRS
BUCKET_ROWS = 32
MOE_TILE = 256
ROW_TILE = 512
ROUTE_TILE = 512
PERM_TILE = 512
NEG_BIG = -1e30
_MEMBER_MAJOR = tuple(EXPERTS_PER_GROUP * g + k for k in range(EXPERTS_PER_GROUP) for g in range(N_GROUPS))
VMEM_LIMIT = 56 * 1024 * 1024

_F32 = jnp.float32
_BF16 = jnp.bfloat16


def _dot(a, b):
    return jnp.dot(a, b, preferred_element_type=_F32)


def _dot_nt(a, b):
    return lax.dot_general(a, b, (((1,), (1,)), ((), ())), preferred_element_type=_F32)


def _params(*sem):
    return pltpu.CompilerParams(dimension_semantics=sem, vmem_limit_bytes=VMEM_LIMIT)


def _rms_mod(x, g, shift, scale):
    y = x * lax.rsqrt(jnp.mean(x * x, axis=-1, keepdims=True) + EPS)
    return (y * g) * (1.0 + scale) + shift


def _pack_pairs(h):
    bits = lax.bitcast_convert_type(h.astype(_BF16).astype(_F32), jnp.int32)
    half = D_MODEL // 2
    lo = lax.shift_right_logical(bits[:, :half], 16)
    hi = bits[:, half:] & jnp.int32(-65536)
    return hi | lo


def _store_packed(ref, words):
    for j in range(PACK_ROWS):
        ref[:, j, :] = words[:, j * LANES:(j + 1) * LANES]


def _load_unpacked(ref):
    chunks = [ref[:, j, :] for j in range(PACK_ROWS)]
    lo = [lax.bitcast_convert_type(lax.shift_left(c, 16), _F32) for c in chunks]
    hi = [lax.bitcast_convert_type(c & jnp.int32(-65536), _F32) for c in chunks]
    return jnp.concatenate(lo + hi, axis=1)


def _ada_kernel(c_ref, w_ref, b_ref, o_ref):
    c = c_ref[...]
    s = c * jax.nn.sigmoid(c)
    o_ref[0] = _dot(s.astype(_BF16), w_ref[0].astype(_BF16)) + b_ref[0]


def _ada(cs, ada_w, ada_b):
    depth, d, n = ada_w.shape
    tn = 1536
    return pl.pallas_call(
        _ada_kernel,
        grid=(depth, n // tn),
        in_specs=[pl.BlockSpec((8, d), lambda l, j: (0, 0)),
                  pl.BlockSpec((1, d, tn), lambda l, j: (l, 0, j)),
                  pl.BlockSpec((1, 1, tn), lambda l, j: (l, 0, j))],
        out_specs=pl.BlockSpec((1, 8, tn), lambda l, j: (l, 0, j)),
        out_shape=jax.ShapeDtypeStruct((depth, 8, n), _F32),
        compiler_params=_params("arbitrary", "arbitrary"),
        name="ada",
    )(cs, ada_w, ada_b.reshape(depth, 1, n))


_KINDS_LATENT = ("rope_k", "id", "norm_k", "id", "rope_q", "silu", "norm_q")
_KINDS_CTX = ("rope_k", "id", "norm_k", "id")


def _rope(t, cos, sin):
    outs = []
    lane = lax.broadcasted_iota(jnp.int32, (1, LANES), 1)
    first = (lane % 32) < 16
    for s in range(t.shape[1] // LANES):
        u = t[:, s * LANES:(s + 1) * LANES]
        partner = jnp.where(first, pltpu.roll(u, LANES - 16, 1), pltpu.roll(u, 16, 1))
        outs.append(u * cos + partner * sin)
    return jnp.concatenate(outs, axis=1)


def _inproj_kernel(x_ref, g_ref, mod_ref, cos_ref, sin_ref, w_ref, qg_ref, kg_ref, bd_ref, o_ref, *, kinds):
    mod = mod_ref[0]
    h = _rms_mod(x_ref[...], g_ref[...], mod[0:1], mod[1:2]).astype(_BF16)
    scale = HEAD_DIM ** -0.5
    for j, kind in enumerate(kinds):
        cols = slice(j * COL_TILE, (j + 1) * COL_TILE)
        acc = _dot(h, w_ref[:, cols])
        if kind == "rope_k":
            acc = _rope(acc, cos_ref[...], sin_ref[...]) * scale
        elif kind == "rope_q":
            acc = _rope(acc, cos_ref[...], sin_ref[...])
        elif kind == "silu":
            acc = acc * jax.nn.sigmoid(acc)
        elif kind in ("norm_k", "norm_q"):
            ms = _dot((acc * acc).astype(_BF16), bd_ref[...])
            g = kg_ref[...] if kind == "norm_k" else qg_ref[...] * scale
            acc = acc * lax.rsqrt(ms + EPS) * g
        o_ref[:, cols] = acc.astype(o_ref.dtype)


def _inproj(x2, g, mods, batch_of_tile, cos, sin, w, qg, kg, bd, kinds, tm):
    t, d = x2.shape
    n = len(kinds) * COL_TILE
    npos = cos.shape[0] // tm
    return pl.pallas_call(
        functools.partial(_inproj_kernel, kinds=kinds),
        grid=(t // tm,),
        in_specs=[pl.BlockSpec((tm, d), lambda i: (i, 0)),
                  pl.BlockSpec((1, d), lambda i: (0, 0)),
                  pl.BlockSpec((1, 6, d), lambda i: (batch_of_tile(i), 0, 0)),
                  pl.BlockSpec((tm, LANES), lambda i: (i % npos, 0)),
                  pl.BlockSpec((tm, LANES), lambda i: (i % npos, 0)),
                  pl.BlockSpec((d, n), lambda i: (0, 0)),
                  pl.BlockSpec((1, COL_TILE), lambda i: (0, 0)),
                  pl.BlockSpec((1, COL_TILE), lambda i: (0, 0)),
                  pl.BlockSpec((COL_TILE, COL_TILE), lambda i: (0, 0))],
        out_specs=pl.BlockSpec((tm, n), lambda i: (i, 0)),
        out_shape=jax.ShapeDtypeStruct((t, n), _BF16),
        compiler_params=_params("arbitrary"),
        name="inproj",
    )(x2, g, mods, cos, sin, w, qg, kg, bd)


def _ret_kernel(q_ref, k_ref, v_ref, gate_ref, kc_ref, vc_ref, lgf_ref, lgb_ref, lgf2_ref, lgb2_ref, g_ref,
                o_ref, sf_scr, sb_scr, *, seq, n_ctx):
    c = RET_CHUNK
    n = seq // c
    lgf, lgb = lgf_ref[0], lgb_ref[0]
    lane = lax.broadcasted_iota(jnp.int32, (1, LANES), 1)
    head0 = lane < HEAD_DIM
    r = lax.broadcasted_iota(jnp.int32, (LANES, LANES), 0)
    cc = lax.broadcasted_iota(jnp.int32, (LANES, LANES), 1)
    same_head = (r < HEAD_DIM) == (cc < HEAD_DIM)
    mean_mat = jnp.where(same_head, 1.0 / HEAD_DIM, 0.0).astype(_BF16)
    pos = lax.broadcasted_iota(jnp.int32, (c, LANES), 0).astype(_F32)
    q_dec_f = jnp.exp(lgf * (pos + 1.0))
    q_dec_b = jnp.exp(lgb * (c - pos))
    k_dec_f = jnp.exp(lgf * (c - 1.0 - pos))
    k_dec_b = jnp.exp(lgb * pos)
    chunk_dec_f = jnp.exp(lgf * c)
    chunk_dec_b = jnp.exp(lgb * c)
    i2 = lax.broadcasted_iota(jnp.int32, (c, 2 * c), 0)
    j2 = lax.broadcasted_iota(jnp.int32, (c, 2 * c), 1)
    diff = (i2 - jnp.where(j2 >= c, j2 - c, j2)).astype(_F32)
    intra = (jnp.where(diff >= 0, jnp.exp(lgf2_ref[0] * jnp.maximum(diff, 0.0)), 0.0)
             + jnp.where(diff <= 0, jnp.exp(lgb2_ref[0] * jnp.maximum(-diff, 0.0)), 0.0))

    def state_update(kf, dec, v):
        kv = _dot((kf * dec).T.astype(_BF16), v)
        return jnp.where(same_head, kv, 0.0)

    m = lax.broadcasted_iota(jnp.int32, (n_ctx, LANES), 0).astype(_F32)
    kc = kc_ref[...].astype(_F32)
    vc = vc_ref[...]
    s_f0 = state_update(kc, jnp.exp(lgf * (n_ctx - 1.0 - m)), vc)
    s_b0 = state_update(kc, jnp.exp(lgb * m), vc)

    def rows_of(i):
        return pl.ds(pl.multiple_of(i * c, c), c)

    def fwd(i, s):
        sf_scr[i] = s.astype(_BF16)
        rows = rows_of(i)
        return s * chunk_dec_f + state_update(k_ref[rows, :].astype(_F32), k_dec_f, v_ref[rows, :])

    def bwd(ii, s):
        i = n - 1 - ii
        sb_scr[i] = s.astype(_BF16)
        rows = rows_of(i)
        return s * chunk_dec_b + state_update(k_ref[rows, :].astype(_F32), k_dec_b, v_ref[rows, :])

    lax.fori_loop(0, n, fwd, s_f0)
    lax.fori_loop(0, n, bwd, s_b0)

    def out(i, carry):
        rows = rows_of(i)
        q, k, v = q_ref[rows, :], k_ref[rows, :], v_ref[rows, :]
        zero = jnp.zeros_like(k)
        k2 = jnp.concatenate([jnp.where(head0, k, zero), jnp.where(head0, zero, k)], axis=0)
        v2 = jnp.concatenate([jnp.where(head0, v, zero), jnp.where(head0, zero, v)], axis=0)
        scores = _dot_nt(q, k2) * intra
        o = _dot(scores.astype(_BF16), v2)
        qf = q.astype(_F32)
        o = o + _dot((qf * q_dec_f).astype(_BF16), sf_scr[i]) + _dot((qf * q_dec_b).astype(_BF16), sb_scr[i])
        ms = _dot((o * o).astype(_BF16), mean_mat)
        y = o * lax.rsqrt(ms + EPS) * g_ref[...] * gate_ref[rows, :].astype(_F32)
        o_ref[rows, :] = y.astype(o_ref.dtype)
        return carry

    lax.fori_loop(0, n, out, 0)


def _retention(proj, proj_c, lgf, lgb, lgf2, lgb2, g2, batch, seq, n_ctx):
    pairs = RET_HEADS // 2
    blk = lambda off: pl.BlockSpec((seq, LANES), lambda b, p: (b, off + p))
    blk_c = lambda off: pl.BlockSpec((n_ctx, LANES), lambda b, p: (b, off + p))
    vec = lambda w: pl.BlockSpec((1, 1, w), lambda b, p: (p, 0, 0))
    k_off, v_off = 0, RET_W // LANES
    q_off, gate_off = AB_KV_W // LANES, (AB_KV_W + RET_W) // LANES
    return pl.pallas_call(
        functools.partial(_ret_kernel, seq=seq, n_ctx=n_ctx),
        grid=(batch, pairs),
        in_specs=[blk(q_off), blk(k_off), blk(v_off), blk(gate_off), blk_c(k_off), blk_c(v_off),
                  vec(LANES), vec(LANES), vec(2 * RET_CHUNK), vec(2 * RET_CHUNK),
                  pl.BlockSpec((1, LANES), lambda b, p: (0, 0))],
        out_specs=pl.BlockSpec((seq, LANES), lambda b, p: (b, p)),
        out_shape=jax.ShapeDtypeStruct((batch * seq, RET_W), _BF16),
        scratch_shapes=[pltpu.VMEM((seq // RET_CHUNK, LANES, LANES), _BF16),
                        pltpu.VMEM((seq // RET_CHUNK, LANES, LANES), _BF16)],
        compiler_params=_params("arbitrary", "arbitrary"),
        name="retention",
    )(proj, proj, proj, proj, proj_c, proj_c, lgf, lgb, lgf2, lgb2, g2)


def _na_kernel(q_ref, k_ref, v_ref, kc_ref, vc_ref, bias_ref, o_ref, *, grid_rows):
    lane = lax.broadcasted_iota(jnp.int32, (1, LANES), 1)
    head0 = lane < HEAD_DIM
    win = NA_ROWS * GRID_W
    kc = kc_ref[...]
    vc = vc_ref[...]

    def row(r, carry):
        r0 = jnp.clip(r - NA_ROWS // 2, 0, grid_rows - NA_ROWS)
        q = q_ref[pl.ds(pl.multiple_of(r * GRID_W, GRID_W), GRID_W), :]
        keys = pl.ds(pl.multiple_of(r0 * GRID_W, GRID_W), win)
        kw, vw = k_ref[keys, :], v_ref[keys, :]
        zero = jnp.zeros_like(q)
        outs = []
        for hh in range(2):
            qm = jnp.where(head0, q, zero) if hh == 0 else jnp.where(head0, zero, q)
            s_loc = _dot_nt(qm, kw) + bias_ref[hh, r - r0]
            s_ctx = _dot_nt(qm, kc)
            mx = jnp.maximum(jnp.max(s_loc, axis=-1, keepdims=True), jnp.max(s_ctx, axis=-1, keepdims=True))
            p_loc = jnp.exp(s_loc - mx)
            p_ctx = jnp.exp(s_ctx - mx)
            den = jnp.sum(p_loc, axis=-1, keepdims=True) + jnp.sum(p_ctx, axis=-1, keepdims=True)
            o = _dot(p_loc.astype(_BF16), vw) + _dot(p_ctx.astype(_BF16), vc)
            outs.append(o / den)
        o_ref[pl.ds(pl.multiple_of(r * GRID_W, GRID_W), GRID_W), :] = jnp.where(head0, outs[0], outs[1]).astype(o_ref.dtype)
        return carry

    lax.fori_loop(0, grid_rows, row, 0)


def _natten(proj, proj_c, bias, batch, seq, n_ctx):
    pairs = NA_HEADS // 2
    blk = lambda off: pl.BlockSpec((seq, LANES), lambda b, p: (b, off + p))
    blk_c = lambda off: pl.BlockSpec((n_ctx, LANES), lambda b, p: (b, off + p))
    k_off, v_off = 2 * RET_W // LANES, (2 * RET_W + NA_W) // LANES
    q_off = (AB_KV_W + 2 * RET_W) // LANES
    return pl.pallas_call(
        functools.partial(_na_kernel, grid_rows=seq // GRID_W),
        grid=(batch, pairs),
        in_specs=[blk(q_off), blk(k_off), blk(v_off), blk_c(k_off), blk_c(v_off),
                  pl.BlockSpec((2, NA_ROWS, GRID_W, NA_ROWS * GRID_W), lambda b, p: (p, 0, 0, 0))],
        out_specs=pl.BlockSpec((seq, LANES), lambda b, p: (b, p)),
        out_shape=jax.ShapeDtypeStruct((batch * seq, NA_W), _BF16),
        compiler_params=_params("arbitrary", "arbitrary"),
        name="natten",
    )(proj, proj, proj, proj_c, proj_c, bias)


def _na_bias_table(rpb):
    w = jnp.arange(GRID_W)
    c0 = jnp.clip(w - NA_COLS // 2, 0, GRID_W - NA_COLS)
    c = jnp.arange(GRID_W)
    inside = (c[None, :] >= c0[:, None]) & (c[None, :] < c0[:, None] + NA_COLS)
    col_idx = jnp.clip(c[None, :] - w[:, None] + (NA_COLS - 1), 0, 2 * NA_COLS - 2)
    didx = jnp.arange(NA_ROWS)
    dr = jnp.arange(NA_ROWS)
    row_idx = dr[None, :] - didx[:, None] + (NA_ROWS - 1)
    tab = rpb[:, row_idx][:, :, :, col_idx]
    tab = jnp.where(inside[None, None, None], tab, NEG_BIG)
    tab = tab.transpose(0, 1, 3, 2, 4)
    return tab.reshape(NA_HEADS, NA_ROWS, GRID_W, NA_ROWS * GRID_W).astype(_F32)


def _post_common(y, x_ref, mod_ref, g_ref, rw_ref, x1_ref, hp_ref, lg_ref):
    mod = mod_ref[0]
    x1 = x_ref[...] + mod[2:3] * y
    x1_ref[...] = x1
    hf = _rms_mod(x1, g_ref[...], mod[3:4], mod[4:5])
    hb = hf.astype(_BF16)
    lg_ref[...] = _dot_nt(rw_ref[...], hb)
    _store_packed(hp_ref, _pack_pairs(hf))


def _post0_kernel(ret_ref, na_ref, wa_ref, wb_ref, x_ref, mod_ref, g_ref, rw_ref, x1_ref, hp_ref, lg_ref):
    y = _dot(ret_ref[...], wa_ref[...]) + _dot(na_ref[...], wb_ref[...])
    _post_common(y, x_ref, mod_ref, g_ref, rw_ref, x1_ref, hp_ref, lg_ref)


def _post1_kernel(u_ref, v_ref, ws_ref, bs_ref, wo_ref, x_ref, mod_ref, g_ref, rw_ref, x1_ref, hp_ref, lg_ref,
                  gated_scr):
    gw = SGU_W // SGU_GROUPS
    for ch in range(u_ref.shape[0] // SGU_CHUNK):
        rows = slice(ch * SGU_CHUNK, (ch + 1) * SGU_CHUNK)
        for g in range(SGU_GROUPS):
            cols = slice(g * gw, (g + 1) * gw)
            mixed = _dot(ws_ref[g], v_ref[rows, cols]) + bs_ref[:, g:g + 1]
            gated_scr[rows, cols] = (u_ref[rows, cols].astype(_F32) * mixed).astype(_BF16)
    y = _dot(gated_scr[...], wo_ref[...])
    _post_common(y, x_ref, mod_ref, g_ref, rw_ref, x1_ref, hp_ref, lg_ref)


def _post_specs(t, tm, seq):
    d = D_MODEL
    tail_in = [pl.BlockSpec((tm, d), lambda i: (i, 0)),
               pl.BlockSpec((1, 6, d), lambda i: (i // (seq // tm), 0, 0)),
               pl.BlockSpec((1, d), lambda i: (0, 0)),
               pl.BlockSpec((N_EXPERTS, d), lambda i: (0, 0))]
    out_specs = [pl.BlockSpec((tm, d), lambda i: (i, 0)),
                 pl.BlockSpec((tm, PACK_ROWS, LANES), lambda i: (i, 0, 0)),
                 pl.BlockSpec((N_EXPERTS, tm), lambda i: (0, i))]
    out_shape = [jax.ShapeDtypeStruct((t, d), _F32),
                 jax.ShapeDtypeStruct((t, PACK_ROWS, LANES), jnp.int32),
                 jax.ShapeDtypeStruct((N_EXPERTS, t), _F32)]
    return tail_in, out_specs, out_shape


def _post0(ret, na, wa, wb, x2, mods, g, rw_t, seq, tm):
    t = x2.shape[0]
    tail_in, out_specs, out_shape = _post_specs(t, tm, seq)
    return pl.pallas_call(
        _post0_kernel,
        grid=(t // tm,),
        in_specs=[pl.BlockSpec((tm, RET_W), lambda i: (i, 0)),
                  pl.BlockSpec((tm, NA_W), lambda i: (i, 0)),
                  pl.BlockSpec((RET_W, D_MODEL), lambda i: (0, 0)),
                  pl.BlockSpec((NA_W, D_MODEL), lambda i: (0, 0))] + tail_in,
        out_specs=out_specs,
        out_shape=out_shape,
        compiler_params=_params("arbitrary"),
        name="post0",
    )(ret, na, wa, wb, x2, mods, g, rw_t)


def _post1(u, v, ws, bs_t, wo, x2, mods, g, rw_t, seq, tm):
    t = x2.shape[0]
    tail_in, out_specs, out_shape = _post_specs(t, tm, seq)
    return pl.pallas_call(
        _post1_kernel,
        grid=(t // tm,),
        in_specs=[pl.BlockSpec((tm, SGU_W), lambda i: (i, 0)),
                  pl.BlockSpec((tm, SGU_W), lambda i: (i, 0)),
                  pl.BlockSpec((SGU_GROUPS, SGU_CHUNK, SGU_CHUNK), lambda i: (0, 0, 0)),
                  pl.BlockSpec((SGU_CHUNK, SGU_GROUPS), lambda i: (0, 0)),
                  pl.BlockSpec((SGU_W, D_MODEL), lambda i: (0, 0))] + tail_in,
        out_specs=out_specs,
        out_shape=out_shape,
        scratch_shapes=[pltpu.VMEM((tm, SGU_W), _BF16)],
        compiler_params=_params("arbitrary"),
        name="post1",
    )(u, v, ws, bs_t, wo, x2, mods, g, rw_t)


def _gelu_tanh(z):
    return 0.5 * z * (1.0 + jnp.tanh(0.7978845608028654 * (z + 0.044715 * (z * z * z))))


def _sgu_in_kernel(x_ref, g_ref, mod_ref, w_ref, b_ref, vg_ref, u_ref, v_ref, v_scr):
    mod = mod_ref[0]
    h = _rms_mod(x_ref[...], g_ref[...], mod[0:1], mod[1:2]).astype(_BF16)
    n_half = SGU_W // COL_TILE
    ssq = jnp.zeros((x_ref.shape[0], 1), _F32)
    for j in range(2 * n_half):
        cols = slice(j * COL_TILE, (j + 1) * COL_TILE)
        z = _gelu_tanh(_dot(h, w_ref[:, cols]) + b_ref[:, cols])
        if j < n_half:
            u_ref[:, cols] = z.astype(u_ref.dtype)
        else:
            vcols = slice((j - n_half) * COL_TILE, (j - n_half + 1) * COL_TILE)
            v_scr[:, vcols] = z
            ssq = ssq + jnp.sum(z * z, axis=-1, keepdims=True)
    inv = lax.rsqrt(ssq * (1.0 / SGU_W) + EPS)
    v_ref[...] = (v_scr[...] * inv * vg_ref[...]).astype(v_ref.dtype)


def _sgu_in(x2, g, mods, w, b, vg, seq, tm):
    t, d = x2.shape
    return pl.pallas_call(
        _sgu_in_kernel,
        grid=(t // tm,),
        in_specs=[pl.BlockSpec((tm, d), lambda i: (i, 0)),
                  pl.BlockSpec((1, d), lambda i: (0, 0)),
                  pl.BlockSpec((1, 6, d), lambda i: (i // (seq // tm), 0, 0)),
                  pl.BlockSpec((d, 2 * SGU_W), lambda i: (0, 0)),
                  pl.BlockSpec((1, 2 * SGU_W), lambda i: (0, 0)),
                  pl.BlockSpec((1, SGU_W), lambda i: (0, 0))],
        out_specs=[pl.BlockSpec((tm, SGU_W), lambda i: (i, 0)),
                   pl.BlockSpec((tm, SGU_W), lambda i: (i, 0))],
        out_shape=[jax.ShapeDtypeStruct((t, SGU_W), _BF16),
                   jax.ShapeDtypeStruct((t, SGU_W), _BF16)],
        scratch_shapes=[pltpu.VMEM((tm, SGU_W), _F32)],
        compiler_params=_params("arbitrary"),
        name="sgu_in",
    )(x2, g, mods, w, b, vg)


def _route_bucket_kernel(lg_ref, bias_ref, bucket_ref, cnt_ref):
    tt = lg_ref.shape[1]
    sel = []
    for k in range(EXPERTS_PER_GROUP):
        rows = slice(k * N_GROUPS, (k + 1) * N_GROUPS)
        sel.append(jax.nn.sigmoid(lg_ref[rows, :]) + bias_ref[rows, :])
    hi01, lo01 = jnp.maximum(sel[0], sel[1]), jnp.minimum(sel[0], sel[1])
    hi23, lo23 = jnp.maximum(sel[2], sel[3]), jnp.minimum(sel[2], sel[3])
    top1 = jnp.maximum(hi01, hi23)
    top2 = jnp.maximum(jnp.minimum(hi01, hi23), jnp.maximum(lo01, lo23))
    score = top1 + top2
    grp = lax.broadcasted_iota(jnp.int32, (N_GROUPS, tt), 0)
    best = jnp.min(jnp.where(score == jnp.max(score, axis=0, keepdims=True), grp, N_GROUPS), axis=0, keepdims=True)
    pick = [jnp.sum(jnp.where(grp == best, s, 0.0), axis=0, keepdims=True) for s in sel]

    def first_argmax(vals):
        mx = jnp.maximum(jnp.maximum(vals[0], vals[1]), jnp.maximum(vals[2], vals[3]))
        return jnp.where(vals[0] == mx, 0, jnp.where(vals[1] == mx, 1, jnp.where(vals[2] == mx, 2, 3)))

    i1 = first_argmax(pick)
    i2 = first_argmax([jnp.where(i1 == k, -jnp.inf, pick[k]) for k in range(EXPERTS_PER_GROUP)])
    lo, hi = jnp.minimum(i1, i2), jnp.maximum(i1, i2)
    base = jnp.where(lo == 0, 0, jnp.where(lo == 1, 3, 5))
    bucket = best * N_PAIRS + base + hi - lo - 1
    bucket_ref[...] = bucket
    onehot = (lax.broadcasted_iota(jnp.int32, (BUCKET_ROWS, tt), 0) == bucket).astype(_F32)

    @pl.when(pl.program_id(0) == 0)
    def _():
        cnt_ref[...] = jnp.zeros_like(cnt_ref)

    cnt_ref[...] += jnp.sum(onehot, axis=1, keepdims=True)


def _route_pos_kernel(bucket_ref, start_ref, pos_ref, run_scr):
    tt = bucket_ref.shape[1]

    @pl.when(pl.program_id(0) == 0)
    def _():
        run_scr[...] = start_ref[...]

    onehot = lax.broadcasted_iota(jnp.int32, (BUCKET_ROWS, tt), 0) == bucket_ref[...]
    before = (lax.broadcasted_iota(jnp.int32, (tt, tt), 0) < lax.broadcasted_iota(jnp.int32, (tt, tt), 1))
    rank = _dot(onehot.astype(_BF16), before.astype(_BF16))
    ohf = onehot.astype(_F32)
    pos = jnp.sum(ohf * (rank + run_scr[...]), axis=0, keepdims=True)
    pos_ref[...] = pos.astype(jnp.int32)
    run_scr[...] += jnp.sum(ohf, axis=1, keepdims=True)


def _route(logits_t, router_bias):
    t = logits_t.shape[1]
    tt = min(ROUTE_TILE, t)
    bucket, cnt = pl.pallas_call(
        _route_bucket_kernel,
        grid=(t // tt,),
        in_specs=[pl.BlockSpec((N_EXPERTS, tt), lambda i: (0, i)),
                  pl.BlockSpec((N_EXPERTS, 1), lambda i: (0, 0))],
        out_specs=[pl.BlockSpec((1, tt), lambda i: (0, i)),
                   pl.BlockSpec((BUCKET_ROWS, 1), lambda i: (0, 0))],
        out_shape=[jax.ShapeDtypeStruct((1, t), jnp.int32),
                   jax.ShapeDtypeStruct((BUCKET_ROWS, 1), _F32)],
        compiler_params=_params("arbitrary"),
        name="route_bucket",
    )(logits_t, router_bias[jnp.asarray(_MEMBER_MAJOR)].reshape(N_EXPERTS, 1).astype(_F32))
    counts = cnt[:, 0].astype(jnp.int32)
    ends = jnp.cumsum(counts)
    starts = ends - counts
    pos = pl.pallas_call(
        _route_pos_kernel,
        grid=(t // tt,),
        in_specs=[pl.BlockSpec((1, tt), lambda i: (0, i)),
                  pl.BlockSpec((BUCKET_ROWS, 1), lambda i: (0, 0))],
        out_specs=pl.BlockSpec((1, tt), lambda i: (0, i)),
        out_shape=jax.ShapeDtypeStruct((1, t), jnp.int32),
        scratch_shapes=[pltpu.VMEM((BUCKET_ROWS, 1), _F32)],
        compiler_params=_params("arbitrary"),
        name="route_pos",
    )(bucket, starts.astype(_F32).reshape(BUCKET_ROWS, 1))
    return pos.reshape(t), starts[:N_BUCKETS], ends[:N_BUCKETS]


def _work_items(starts, ends, t, tile):
    n_tiles = t // tile
    n_items = n_tiles + N_BUCKETS - 1
    first_tile = starts // tile
    last_tile = jnp.where(ends > starts, (ends - 1) // tile, first_tile - 1)
    per_bucket = jnp.maximum(last_tile - first_tile + 1, 0)
    item_end = jnp.cumsum(per_bucket)
    item_start = item_end - per_bucket
    w = jnp.arange(n_items)
    bucket = jnp.minimum(jnp.sum(w[:, None] >= item_end[None, :], axis=1), N_BUCKETS - 1)
    valid = w < item_end[-1]
    tile_idx = jnp.where(valid, first_tile[bucket] + w - item_start[bucket], n_tiles - 1)
    row_lo = jnp.where(valid, jnp.maximum(starts[bucket], tile_idx * tile), 0)
    row_hi = jnp.where(valid, jnp.minimum(ends[bucket], (tile_idx + 1) * tile), 0)
    prev_tile = jnp.concatenate([jnp.full((1,), -1, tile_idx.dtype), tile_idx[:-1]])
    first = (tile_idx != prev_tile).astype(jnp.int32)
    grp, pair = bucket // N_PAIRS, bucket % N_PAIRS
    pair_lo = jnp.array([0, 0, 0, 1, 1, 2], jnp.int32)[pair]
    pair_hi = jnp.array([1, 2, 3, 2, 3, 3], jnp.int32)[pair]
    e_lo = grp * EXPERTS_PER_GROUP + pair_lo
    e_hi = grp * EXPERTS_PER_GROUP + pair_hi
    as_i32 = lambda a: a.astype(jnp.int32)
    return tuple(map(as_i32, (tile_idx, e_lo, e_hi, row_lo, row_hi, first)))


def _row_copies(n_rows, make_copy):
    unroll = 8

    def issue(g, carry):
        for u in range(unroll):
            make_copy(g * unroll + u).start()
        return carry

    def drain(g, carry):
        for u in range(unroll):
            make_copy(g * unroll + u).wait()
        return carry

    lax.fori_loop(0, n_rows // unroll, issue, 0)
    lax.fori_loop(0, n_rows // unroll, drain, 0)


def _scatter_kernel(pos_ref, hp_ref, out_ref, sem):
    tm = hp_ref.shape[0]
    base = pl.program_id(0) * tm
    _row_copies(tm, lambda i: pltpu.make_async_copy(hp_ref.at[i], out_ref.at[pos_ref[base + i]], sem))


def _scatter_rows(pos, hp):
    t = hp.shape[0]
    tm = min(PERM_TILE, t)
    return pl.pallas_call(
        _scatter_kernel,
        grid_spec=pltpu.PrefetchScalarGridSpec(
            num_scalar_prefetch=1,
            grid=(t // tm,),
            in_specs=[pl.BlockSpec((tm, PACK_ROWS, LANES), lambda i, pos: (i, 0, 0))],
            out_specs=pl.BlockSpec(memory_space=pl.ANY),
            scratch_shapes=[pltpu.SemaphoreType.DMA(())]),
        out_shape=jax.ShapeDtypeStruct(hp.shape, hp.dtype),
        compiler_params=_params("arbitrary"),
        name="scatter_rows",
    )(pos, hp)


def _final_kernel(pos_ref, ys_ref, x1_ref, mod_ref, o_ref, buf, sem):
    tm = x1_ref.shape[0]
    base = pl.program_id(0) * tm
    _row_copies(tm, lambda i: pltpu.make_async_copy(ys_ref.at[pos_ref[base + i]], buf.at[i], sem))
    o_ref[...] = x1_ref[...] + mod_ref[0][5:6] * _load_unpacked(buf)


def _final(pos, ys, x1, mods, seq):
    t, d = x1.shape
    tm = min(PERM_TILE, seq)
    return pl.pallas_call(
        _final_kernel,
        grid_spec=pltpu.PrefetchScalarGridSpec(
            num_scalar_prefetch=1,
            grid=(t // tm,),
            in_specs=[pl.BlockSpec(memory_space=pl.ANY),
                      pl.BlockSpec((tm, d), lambda i, pos: (i, 0)),
                      pl.BlockSpec((1, 6, d), lambda i, pos: (i // (seq // tm), 0, 0))],
            out_specs=pl.BlockSpec((tm, d), lambda i, pos: (i, 0)),
            scratch_shapes=[pltpu.VMEM((tm, PACK_ROWS, LANES), jnp.int32),
                            pltpu.SemaphoreType.DMA(())]),
        out_shape=jax.ShapeDtypeStruct((t, d), _F32),
        compiler_params=_params("arbitrary"),
        name="moe_combine",
    )(pos, ys, x1, mods)


def _moe_kernel(tile_ref, elo_ref, ehi_ref, rlo_ref, rhi_ref, first_ref,
                hs_ref, rw_ref, wg_lo, wu_lo, wd_lo, wg_hi, wu_hi, wd_hi, o_ref, acc_scr):
    w = pl.program_id(0)
    tile = hs_ref.shape[0]

    @pl.when(first_ref[w] == 1)
    def _():
        acc_scr[...] = jnp.zeros_like(acc_scr)

    @pl.when(rhi_ref[w] > rlo_ref[w])
    def _():
        h = _load_unpacked(hs_ref).astype(_BF16)
        aff = jax.nn.sigmoid(_dot(h, rw_ref[...]))
        lane = lax.broadcasted_iota(jnp.int32, aff.shape, 1)
        a_lo = jnp.sum(jnp.where(lane == elo_ref[w], aff, 0.0), axis=-1, keepdims=True)
        a_hi = jnp.sum(jnp.where(lane == ehi_ref[w], aff, 0.0), axis=-1, keepdims=True)
        den = a_lo + a_hi

        def expert(wg, wu, wd, weight):
            g = _dot(h, wg[0])
            u = _dot(h, wu[0])
            act = (g * jax.nn.sigmoid(g)) * u
            return _dot(act.astype(_BF16), wd[0]) * weight

        y = expert(wg_lo, wu_lo, wd_lo, a_lo / den) + expert(wg_hi, wu_hi, wd_hi, a_hi / den)
        rows = tile_ref[w] * tile + lax.broadcasted_iota(jnp.int32, (tile, 1), 0)
        mine = (rows >= rlo_ref[w]) & (rows < rhi_ref[w])
        acc_scr[...] = jnp.where(mine, y, acc_scr[...])
        _store_packed(o_ref, _pack_pairs(acc_scr[...]))


def _moe(items, hs, rw_pad, w_gate, w_up, w_down):
    t = hs.shape[0]
    tile = min(MOE_TILE, t)
    n_items = items[0].shape[0]
    d, ew = D_MODEL, EXPERT_W
    lo = lambda w, tl, elo, ehi, rlo, rhi, fr: (elo[w], 0, 0)
    hi = lambda w, tl, elo, ehi, rlo, rhi, fr: (ehi[w], 0, 0)
    rows = lambda w, tl, elo, ehi, rlo, rhi, fr: (tl[w], 0, 0)
    return pl.pallas_call(
        _moe_kernel,
        grid_spec=pltpu.PrefetchScalarGridSpec(
            num_scalar_prefetch=6,
            grid=(n_items,),
            in_specs=[pl.BlockSpec((tile, PACK_ROWS, LANES), rows),
                      pl.BlockSpec((d, LANES), lambda w, *_: (0, 0)),
                      pl.BlockSpec((1, d, ew), lo), pl.BlockSpec((1, d, ew), lo), pl.BlockSpec((1, ew, d), lo),
                      pl.BlockSpec((1, d, ew), hi), pl.BlockSpec((1, d, ew), hi), pl.BlockSpec((1, ew, d), hi)],
            out_specs=pl.BlockSpec((tile, PACK_ROWS, LANES), rows),
            scratch_shapes=[pltpu.VMEM((tile, d), _F32)]),
        out_shape=jax.ShapeDtypeStruct(hs.shape, jnp.int32),
        compiler_params=_params("arbitrary"),
        name="moe_experts",
    )(*items, hs, rw_pad, w_gate, w_up, w_down, w_gate, w_up, w_down)


def _moe_layer(x1, hp, logits_t, mods, router_bias, rw_pad, w_gate, w_up, w_down, seq):
    t = x1.shape[0]
    pos, starts, ends = _route(logits_t, router_bias)
    hs = _scatter_rows(pos, hp)
    items = _work_items(starts, ends, t, min(MOE_TILE, t))
    ys = _moe(items, hs, rw_pad, w_gate, w_up, w_down)
    return _final(pos, ys, x1, mods, seq)


def _rope_tables(seq):
    quarter = HEAD_DIM // 4
    inv_freq = ROPE_BASE ** (-jnp.arange(quarter, dtype=_F32) / quarter)
    tok = jnp.arange(seq)
    ang_r = (tok // GRID_W).astype(_F32)[:, None] * inv_freq
    ang_c = (tok % GRID_W).astype(_F32)[:, None] * inv_freq
    cos = jnp.concatenate([jnp.cos(ang_r), jnp.cos(ang_r), jnp.cos(ang_c), jnp.cos(ang_c)], axis=-1)
    sin = jnp.concatenate([-jnp.sin(ang_r), jnp.sin(ang_r), -jnp.sin(ang_c), jnp.sin(ang_c)], axis=-1)
    return jnp.tile(cos, (1, 2)), jnp.tile(sin, (1, 2))


def _per_pair_lanes(v, width):
    return jnp.repeat(v.reshape(-1, 2), width, axis=1).reshape(-1, 1, 2 * width)


def kernel(x, c, ctx, c_ctx, ada_w, ada_b, norm_mix_g, norm_ffn_g, router_w, router_bias, moe_w_gate, moe_w_up,
           moe_w_down, ab_w_in, ab_w_out, ret_decay, ret_norm_g, na_q_g, na_k_g, na_rpb, sgu_w_in, sgu_b_in,
           sgu_norm_g, sgu_w_s, sgu_b_s, sgu_w_out):
    batch, seq, d = x.shape
    n_ctx = ctx.shape[1]
    assert d == D_MODEL and batch <= 7 and seq % ROW_TILE == 0 and seq % (NA_ROWS * GRID_W) == 0
    t = batch * seq
    tm = ROW_TILE
    x2 = x.reshape(t, d)
    ctx2 = ctx.reshape(batch * n_ctx, d)

    cs = jnp.zeros((8, d), _F32).at[:batch].set(c).at[batch].set(c_ctx)
    mods = _ada(cs, ada_w, ada_b).reshape(ada_w.shape[0], 8, 6, d)

    rw_t = router_w.T[jnp.asarray(_MEMBER_MAJOR)].astype(_BF16)
    rw_pad = jnp.zeros((d, LANES), _BF16).at[:, :N_EXPERTS].set(router_w.astype(_BF16))
    row = lambda v: v.reshape(1, -1).astype(_F32)

    w_in = ab_w_in[0].astype(_BF16)
    cos, sin = _rope_tables(seq)
    ones, zeros = jnp.ones((n_ctx, LANES), _F32), jnp.zeros((n_ctx, LANES), _F32)
    qg = row(jnp.tile(na_q_g[0], NA_HEADS))
    kg = row(jnp.tile(na_k_g[0], NA_HEADS))
    rr = jnp.arange(COL_TILE) // HEAD_DIM
    bd = jnp.where(rr[:, None] == rr[None, :], 1.0 / HEAD_DIM, 0.0).astype(_BF16)
    g_mix = row(norm_mix_g[0])
    proj = _inproj(x2, g_mix, mods[0], lambda i: i // (seq // tm), cos, sin, w_in, qg, kg, bd, _KINDS_LATENT, tm)
    tmc = n_ctx
    proj_c = _inproj(ctx2, g_mix, mods[0], lambda i: batch, ones, zeros, w_in[:, :AB_KV_W], qg, kg, bd,
                     _KINDS_CTX, tmc)

    log_gamma = -jax.nn.softplus(ret_decay[0].astype(_F32))
    lgf, lgb = _per_pair_lanes(log_gamma[0], HEAD_DIM), _per_pair_lanes(log_gamma[1], HEAD_DIM)
    lgf2, lgb2 = _per_pair_lanes(log_gamma[0], RET_CHUNK), _per_pair_lanes(log_gamma[1], RET_CHUNK)
    ret = _retention(proj, proj_c, lgf, lgb, lgf2, lgb2, row(jnp.tile(ret_norm_g[0], 2)), batch, seq, n_ctx)
    na = _natten(proj, proj_c, _na_bias_table(na_rpb[0]), batch, seq, n_ctx)

    w_out = ab_w_out[0].astype(_BF16)
    x1, hp, logits_t = _post0(ret, na, w_out[:RET_W], w_out[RET_W:], x2, mods[0], row(norm_ffn_g[0]), rw_t, seq, tm)
    x2 = _moe_layer(x1, hp, logits_t, mods[0], router_bias, rw_pad, moe_w_gate[0].astype(_BF16),
                    moe_w_up[0].astype(_BF16), moe_w_down[0].astype(_BF16), seq)

    tm1 = 256
    u, v = _sgu_in(x2, row(norm_mix_g[1]), mods[1], sgu_w_in[0].astype(_BF16), row(sgu_b_in[0]),
                   row(sgu_norm_g[0]), seq, tm1)
    x1, hp, logits_t = _post1(u, v, sgu_w_s[0].astype(_BF16), sgu_b_s[0].T.astype(_F32), sgu_w_out[0].astype(_BF16),
                              x2, mods[1], row(norm_ffn_g[1]), rw_t, seq, tm1)
    x2 = _moe_layer(x1, hp, logits_t, mods[1], router_bias, rw_pad, moe_w_gate[1].astype(_BF16),
                    moe_w_up[1].astype(_BF16), moe_w_down[1].astype(_BF16), seq)
    return x2.reshape(batch, seq, d)
```
